```python
import math
import jax, jax.numpy as jnp
from jax import lax
import numpy as np

D_MODEL = 1024
BATCH = 8
SEQ = 8192
DEPTH = 1

LRU_WIDTH = D_MODEL
LRU_BLOCKS = 16
LRU_BLOCK_W = LRU_WIDTH // LRU_BLOCKS
LRU_C = 8.0
CONV_K = 4
SSD_HEAD_DIM = 64
SSD_INNER = D_MODEL
SSD_HEADS = SSD_INNER // SSD_HEAD_DIM
SSD_GROUPS = 2
SSD_HPG = SSD_HEADS // SSD_GROUPS
SSD_STATE = 128
SSD_CHUNK = 128
MIX_WIDTH = LRU_WIDTH + SSD_INNER
SPLITS = (LRU_WIDTH, LRU_WIDTH, SSD_INNER, SSD_INNER,
          SSD_GROUPS * SSD_STATE, SSD_GROUPS * SSD_STATE, SSD_HEADS)
IN_COLS = sum(SPLITS)
SSD_CONV_CH = SSD_INNER + 2 * SSD_GROUPS * SSD_STATE
D_FF = -(-8 * D_MODEL // (3 * 256)) * 256
EPS = 1e-6

kernel_name = "hymba_style_rglru_ssd_hybrid"


def rms_norm(x, w):
    xf = x.astype(jnp.float32)
    y = xf * lax.rsqrt(jnp.mean(xf * xf, axis=-1, keepdims=True) + EPS)
    return (y * w.astype(jnp.float32)).astype(x.dtype)


def causal_depthwise_conv(x, w, b):
    K = w.shape[0]
    T = x.shape[1]
    xp = jnp.pad(x, ((0, 0), (K - 1, 0), (0, 0)))
    y = b + xp[:, 0:T] * w[0]
    for k in range(1, K):
        y = y + xp[:, k:k + T] * w[k]
    return y


def rg_lru(x, w_a, b_a, w_x, b_x, lam):
    bsz, T, W = x.shape
    xf = x.astype(jnp.float32)
    xb = xf.reshape(bsz, T, LRU_BLOCKS, LRU_BLOCK_W)
    r = jax.nn.sigmoid(jnp.einsum("btki,kij->btkj", xb, w_a.astype(jnp.float32)).reshape(bsz, T, W) + b_a.astype(jnp.float32))
    i = jax.nn.sigmoid(jnp.einsum("btki,kij->btkj", xb, w_x.astype(jnp.float32)).reshape(bsz, T, W) + b_x.astype(jnp.float32))
    log_a = -LRU_C * r * jax.nn.softplus(-lam.astype(jnp.float32))
    a = jnp.exp(log_a)
    u = jnp.sqrt(-jnp.expm1(2.0 * log_a)) * (i * xf)

    def combine(left, right):
        a1, b1 = left
        a2, b2 = right
        return a1 * a2, a2 * b1 + b2

    _, h = lax.associative_scan(combine, (a, u), axis=1)
    return h.astype(x.dtype)


def segsum(a):
    L = a.shape[-1]
    cs = jnp.cumsum(a, axis=-1)
    diff = cs[..., :, None] - cs[..., None, :]
    mask = jnp.tril(jnp.ones((L, L), dtype=bool))
    return jnp.where(mask, diff, -jnp.inf)


def ssd_chunked(xs, a, Bm, Cm):
    b, t, g, e, p = xs.shape
    n = Bm.shape[-1]
    c = t // SSD_CHUNK
    xs = xs.reshape(b, c, SSD_CHUNK, g, e, p)
    Bm = Bm.reshape(b, c, SSD_CHUNK, g, n)
    Cm = Cm.reshape(b, c, SSD_CHUNK, g, n)
    a = a.reshape(b, c, SSD_CHUNK, g, e).transpose(0, 3, 4, 1, 2)
    a_cs = jnp.cumsum(a, axis=-1)
    Lmat = jnp.exp(segsum(a))
    scores = jnp.einsum("bclgn,bcsgn->bgcls", Cm, Bm)
    M = scores[:, :, None] * Lmat
    y_diag = jnp.einsum("bgecls,bcsgep->bclgep", M, xs)
    decay_states = jnp.exp(a_cs[..., -1:] - a_cs)
    states = jnp.einsum("bclgn,bgecl,bclgep->bcgepn", Bm, decay_states, xs)
    chunk_a = jnp.pad(a_cs[..., -1], ((0, 0), (0, 0), (0, 0), (1, 0)))
    decay_chunk = jnp.exp(segsum(chunk_a))
    states = jnp.concatenate([jnp.zeros_like(states[:, :1]), states], axis=1)
    prev_states = jnp.einsum("bgezc,bcgepn->bzgepn", decay_chunk, states)[:, :-1]
    y_off = jnp.einsum("bclgn,bcgepn,bgecl->bclgep", Cm, prev_states, jnp.exp(a_cs))
    return (y_diag + y_off).reshape(b, t, g, e, p)


def setup_inputs(seed: int = 0) -> dict:
    key = jax.random.key(seed)
    ks = jax.random.split(key, 24)
    f32 = jnp.float32
    L = DEPTH

    def nrm(k, shape, scale):
        return jax.random.normal(k, shape, f32) * scale

    def gain(k, shape):
        return 1.0 + 0.05 * jax.random.normal(k, shape, f32)

    x = jax.random.normal(ks[0], (BATCH, SEQ, D_MODEL), f32)
    a_init = jax.random.uniform(ks[9], (L, LRU_WIDTH), f32, 0.9, 0.999)
    s = a_init ** (1.0 / LRU_C)
    lru_lambda = jnp.log(s) - jnp.log1p(-s)
    dt0 = jnp.exp(jax.random.uniform(ks[13], (L, SSD_HEADS), f32, math.log(1e-3), math.log(1e-1)))
    ssd_dt_bias = dt0 + jnp.log(-jnp.expm1(-dt0))
    ssd_a_log = jnp.log(jax.random.uniform(ks[14], (L, SSD_HEADS), f32, 1.0, 16.0))
    return {
        "x": x,
        "pre_mix_norm": gain(ks[1], (L, D_MODEL)),
        "w_in": nrm(ks[2], (L, D_MODEL, IN_COLS), D_MODEL ** -0.5),
        "lru_conv_w": nrm(ks[3], (L, CONV_K, LRU_WIDTH), CONV_K ** -0.5),
        "lru_conv_b": nrm(ks[4], (L, LRU_WIDTH), 0.01),
        "lru_wa": nrm(ks[5], (L, LRU_BLOCKS, LRU_BLOCK_W, LRU_BLOCK_W), LRU_BLOCK_W ** -0.5),
        "lru_ba": nrm(ks[6], (L, LRU_WIDTH), 0.01),
        "lru_wx": nrm(ks[7], (L, LRU_BLOCKS, LRU_BLOCK_W, LRU_BLOCK_W), LRU_BLOCK_W ** -0.5),
        "lru_bx": nrm(ks[8], (L, LRU_WIDTH), 0.01),
        "lru_lambda": lru_lambda,
        "lru_out_norm": gain(ks[10], (L, LRU_WIDTH)),
        "ssd_conv_w": nrm(ks[11], (L, CONV_K, SSD_CONV_CH), CONV_K ** -0.5),
        "ssd_conv_b": nrm(ks[12], (L, SSD_CONV_CH), 0.01),
        "ssd_dt_bias": ssd_dt_bias,
        "ssd_a_log": ssd_a_log,
        "ssd_d": gain(ks[15], (L, SSD_HEADS)),
        "ssd_out_norm": gain(ks[16], (L, SSD_INNER)),
        "w_out": nrm(ks[17], (L, MIX_WIDTH, D_MODEL), MIX_WIDTH ** -0.5),
        "post_mix_norm": gain(ks[18], (L, D_MODEL)),
        "pre_ffn_norm": gain(ks[19], (L, D_MODEL)),
        "w_gate": nrm(ks[20], (L, D_MODEL, D_FF), D_MODEL ** -0.5),
        "w_up": nrm(ks[21], (L, D_MODEL, D_FF), D_MODEL ** -0.5),
        "w_down": nrm(ks[22], (L, D_FF, D_MODEL), D_FF ** -0.5),
        "post_ffn_norm": gain(ks[23], (L, D_MODEL)),
    }


def reference(x, pre_mix_norm, w_in, lru_conv_w, lru_conv_b, lru_wa, lru_ba, lru_wx, lru_bx,
              lru_lambda, lru_out_norm, ssd_conv_w, ssd_conv_b, ssd_dt_bias, ssd_a_log, ssd_d,
              ssd_out_norm, w_out, post_mix_norm, pre_ffn_norm, w_gate, w_up, w_down, post_ffn_norm):
    bsz, T, _ = x.shape
    offs = np.cumsum((0,) + SPLITS)
    for li in range(DEPTH):
        h = rms_norm(x, pre_mix_norm[li])
        proj = h @ w_in[li]
        lru_x = proj[..., offs[0]:offs[1]]
        lru_gate = proj[..., offs[1]:offs[2]]
        ssd_z = proj[..., offs[2]:offs[3]]
        ssd_xbc = proj[..., offs[3]:offs[6]]
        ssd_dt = proj[..., offs[6]:offs[7]]

        lx = causal_depthwise_conv(lru_x, lru_conv_w[li], lru_conv_b[li])
        lh = rg_lru(lx, lru_wa[li], lru_ba[li], lru_wx[li], lru_bx[li], lru_lambda[li])
        y_lru = rms_norm(lh * jax.nn.gelu(lru_gate), lru_out_norm[li])

        xbc = jax.nn.silu(causal_depthwise_conv(ssd_xbc, ssd_conv_w[li], ssd_conv_b[li]))
        sx = xbc[..., :SSD_INNER].astype(jnp.float32).reshape(bsz, T, SSD_GROUPS, SSD_HPG, SSD_HEAD_DIM)
        sB = xbc[..., SSD_INNER:SSD_INNER + SSD_GROUPS * SSD_STATE].astype(jnp.float32).reshape(bsz, T, SSD_GROUPS, SSD_STATE)
        sC = xbc[..., SSD_INNER + SSD_GROUPS * SSD_STATE:].astype(jnp.float32).reshape(bsz, T, SSD_GROUPS, SSD_STATE)
        dt = jax.nn.softplus(ssd_dt.astype(jnp.float32) + ssd_dt_bias[li].astype(jnp.float32))
        dt = dt.reshape(bsz, T, SSD_GROUPS, SSD_HPG)
        A = -jnp.exp(ssd_a_log[li].astype(jnp.float32)).reshape(SSD_GROUPS, SSD_HPG)
        y = ssd_chunked(sx * dt[..., None], dt * A, sB, sC)
        y = y + ssd_d[li].astype(jnp.float32).reshape(SSD_GROUPS, SSD_HPG)[..., None] * sx
        y = y.reshape(bsz, T, SSD_INNER).astype(x.dtype)
        y_ssd = rms_norm(y * jax.nn.silu(ssd_z), ssd_out_norm[li])

        mix = jnp.concatenate([y_lru, y_ssd], axis=-1) @ w_out[li]
        x = x + rms_norm(mix, post_mix_norm[li])

        h = rms_norm(x, pre_ffn_norm[li])
        f = (jax.nn.silu(h @ w_gate[li]) * (h @ w_up[li])) @ w_down[li]
        x = x + rms_norm(f, post_ffn_norm[li])
    return x
```

```python
import functools
import math

import jax
import jax.numpy as jnp
from jax import lax
from jax.experimental import pallas as pl
from jax.experimental.pallas import tpu as pltpu

D_MODEL = 1024
LRU_WIDTH = D_MODEL
LRU_BLOCKS = 16
LRU_BLOCK_W = LRU_WIDTH // LRU_BLOCKS
LRU_C = 8.0
CONV_K = 4
SSD_HEAD_DIM = 64
SSD_INNER = D_MODEL
SSD_HEADS = SSD_INNER // SSD_HEAD_DIM
SSD_GROUPS = 2
SSD_STATE = 128
SSD_CHUNK = 128
SSD_BC = SSD_GROUPS * SSD_STATE
SSD_CONV_CH = SSD_INNER + 2 * SSD_BC
D_FF = -(-8 * D_MODEL // (3 * 256)) * 256
EPS = 1e-6

V7X_LANES = 128
V7X_SUBLANES = 8
V7X_MXU_DIM = 256
V7X_VMEM_BYTES = 64 * 1024 * 1024

LRU_GROUP = V7X_MXU_DIM // LRU_BLOCK_W
LRU_NGROUPS = LRU_BLOCKS // LRU_GROUP
HEADS_PER_TILE = V7X_LANES // SSD_HEAD_DIM
SSD_PAIRS = SSD_HEADS // HEADS_PER_TILE
PAIRS_PER_GROUP = SSD_PAIRS // SSD_GROUPS

MIX_BLOCK_T = 256
FFN_BLOCK_M = 256

BF16 = jnp.bfloat16
F32 = jnp.float32


def _rms_norm(x, w):
    y = x * lax.rsqrt(jnp.mean(x * x, axis=-1, keepdims=True) + EPS)
    return y * w


def _softplus(x):
    return jnp.maximum(x, 0.0) + jnp.log1p(jnp.exp(-jnp.abs(x)))


def _mm(a, b):
    return jnp.dot(a, b, preferred_element_type=F32)


def _split3(x):
    hi = x.astype(BF16)
    r1 = x - hi.astype(F32)
    mid = r1.astype(BF16)
    lo = (r1 - mid.astype(F32)).astype(BF16)
    return hi, mid, lo


def _causal_conv(buf_ref, w_ref, b_ref, n):
    base = V7X_SUBLANES - (CONV_K - 1)
    y = b_ref[...] + buf_ref[pl.ds(base, n), :] * w_ref[0:1, :]
    for k in range(1, CONV_K):
        y = y + buf_ref[pl.ds(base + k, n), :] * w_ref[k:k + 1, :]
    return y


def _lru_scan(a, u, h0):
    n, w = a.shape
    nt = n // V7X_SUBLANES
    a3 = a.reshape(nt, V7X_SUBLANES, w)
    u3 = u.reshape(nt, V7X_SUBLANES, w)
    row = lax.broadcasted_iota(jnp.int32, (1, V7X_SUBLANES, w), 1)
    d = 1
    while d < V7X_SUBLANES:
        m = row >= d
        a_sh = pltpu.roll(a3, d, axis=1)
        u_sh = pltpu.roll(u3, d, axis=1)
        u3 = u3 + a3 * jnp.where(m, u_sh, 0.0)
        a3 = a3 * jnp.where(m, a_sh, 1.0)
        d *= 2
    tiles = []
    h = h0
    for i in range(nt):
        ht = u3[i] + a3[i] * h
        tiles.append(ht)
        h = ht[V7X_SUBLANES - 1:V7X_SUBLANES, :]
    return jnp.concatenate(tiles, axis=0), h


def _mixer_kernel(x_ref, pre_w_ref, w_lx_ref, w_lg_ref, w_z_ref, w_xbc_ref, w_dt_ref,
                  lconv_w_ref, lconv_b_ref, w_gates_ref, ba_ref, bx_ref, lam_ref, lnorm_ref,
                  sconv_w_ref, sconv_b_ref, dtb_ref, alog_ref, dfull_ref, snorm_ref,
                  wout_l_ref, wout_s_ref, post_w_ref,
                  o_ref,
                  lxbuf, xbcbuf, hcarry, state):
    n = x_ref.shape[1]
    j = pl.program_id(1)
    tail = pl.ds(n, V7X_SUBLANES)
    head = pl.ds(0, V7X_SUBLANES)

    @pl.when(j == 0)
    def _():
        lxbuf[head, :] = jnp.zeros((V7X_SUBLANES, LRU_WIDTH), F32)
        xbcbuf[head, :] = jnp.zeros((V7X_SUBLANES, SSD_CONV_CH), F32)
        hcarry[...] = jnp.zeros_like(hcarry)
        state[...] = jnp.zeros_like(state)

    @pl.when(j > 0)
    def _():
        lxbuf[head, :] = lxbuf[tail, :]
        xbcbuf[head, :] = xbcbuf[tail, :]

    xb = x_ref[0]
    hb = _rms_norm(xb, pre_w_ref[...]).astype(BF16)

    body = pl.ds(V7X_SUBLANES, n)
    lxbuf[body, :] = _mm(hb, w_lx_ref[...])
    xbcbuf[body, :] = _mm(hb, w_xbc_ref[...])
    gate = _mm(hb, w_lg_ref[...])
    z = _mm(hb, w_z_ref[...])
    dt_raw = _mm(hb, w_dt_ref[...])

    lx = _causal_conv(lxbuf, lconv_w_ref, lconv_b_ref, n)
    lxb = lx.astype(BF16)
    gates = []
    for q in range(LRU_NGROUPS):
        lanes = slice(q * V7X_MXU_DIM, (q + 1) * V7X_MXU_DIM)
        gates.append(_mm(lxb[:, lanes], w_gates_ref[q]))
    ra = jnp.concatenate([g[:, :V7X_MXU_DIM] for g in gates], axis=1)
    rx = jnp.concatenate([g[:, V7X_MXU_DIM:] for g in gates], axis=1)
    r = jax.nn.sigmoid(ra + ba_ref[...])
    i_g = jax.nn.sigmoid(rx + bx_ref[...])
    log_a = (-LRU_C * r) * _softplus(-lam_ref[...])
    a = jnp.exp(log_a)
    one_minus_a2 = -jnp.tanh(log_a) * (a * a + 1.0)
    u = jnp.sqrt(one_minus_a2) * (i_g * lx)
    hseq, hlast = _lru_scan(a, u, hcarry[0:1, :])
    hcarry[...] = jnp.broadcast_to(hlast, hcarry.shape)
    y_lru = _rms_norm(hseq * jax.nn.gelu(gate), lnorm_ref[...]).astype(BF16)

    xbc = jax.nn.silu(_causal_conv(xbcbuf, sconv_w_ref, sconv_b_ref, n))
    sx = xbc[:, :SSD_INNER]
    sB = xbc[:, SSD_INNER:SSD_INNER + SSD_BC]
    sC = xbc[:, SSD_INNER + SSD_BC:]
    dt = _softplus(dt_raw + dtb_ref[...])
    head_lane = lax.broadcasted_iota(jnp.int32, (1, V7X_LANES), 1)
    a_neg = jnp.where(head_lane < SSD_HEADS, -jnp.exp(alog_ref[...]), 0.0)
    adt = dt * a_neg

    L = SSD_CHUNK
    ri = lax.broadcasted_iota(jnp.int32, (L, L), 0)
    ci = lax.broadcasted_iota(jnp.int32, (L, L), 1)
    tri = ri >= ci
    tri_b = tri.astype(BF16)
    lane = lax.broadcasted_iota(jnp.int32, (1, V7X_LANES), 1)
    lo_half = lane < SSD_HEAD_DIM

    y_chunks = []
    for c in range(n // L):
        rows = slice(c * L, (c + 1) * L)
        dt_c = dt[rows]
        hi, mid, lo = _split3(adt[rows])
        cs = _mm(tri_b, hi) + _mm(tri_b, mid) + _mm(tri_b, lo)
        cs_last = cs[L - 1:L, :]
        wcol = jnp.exp(cs_last - cs) * dt_c
        csT = cs.T
        dtT = dt_c.T
        wT = wcol.T
        sx_c = sx[rows]
        y_pairs = []
        for g in range(SSD_GROUPS):
            gl = slice(g * SSD_STATE, (g + 1) * SSD_STATE)
            B_g = sB[rows, gl]
            C_g = sC[rows, gl]
            BT_g = B_g.T
            scores = _mm(C_g.astype(BF16), BT_g.astype(BF16))
            for pp in range(PAIRS_PER_GROUP):
                p = g * PAIRS_PER_GROUP + pp
                sx_p = sx_c[:, p * V7X_LANES:(p + 1) * V7X_LANES]
                sx_lo = jnp.where(lo_half, sx_p, 0.0).astype(BF16)
                sx_hi = jnp.where(lo_half, 0.0, sx_p).astype(BF16)
                st = state[p]
                st_lo = jnp.where(lo_half, st, 0.0).astype(BF16)
                st_hi = jnp.where(lo_half, 0.0, st).astype(BF16)
                lhs_y, lhs_s, cdec = [], [], []
                for e in (HEADS_PER_TILE * p, HEADS_PER_TILE * p + 1):
                    colb = jnp.broadcast_to(cs[:, e:e + 1], (L, L))
                    lm = jnp.exp(colb - csT[e:e + 1, :])
                    m_e = jnp.where(tri, scores * lm, 0.0) * dtT[e:e + 1, :]
                    ecol = jnp.exp(colb)
                    lhs_y += [m_e.astype(BF16), (C_g * ecol).astype(BF16)]
                    lhs_s.append((BT_g * wT[e:e + 1, :]).astype(BF16))
                    cdec.append(ecol[L - 1:L, :])
                rhs_y = jnp.concatenate([sx_lo, st_lo, sx_hi, st_hi], axis=0)
                y_pairs.append(_mm(jnp.concatenate(lhs_y, axis=1), rhs_y))
                new = _mm(jnp.concatenate(lhs_s, axis=1), jnp.concatenate([sx_lo, sx_hi], axis=0))
                state[p] = st * jnp.where(lo_half, cdec[0], cdec[1]) + new
        y_chunks.append(jnp.concatenate(y_pairs, axis=1))
    y = jnp.concatenate(y_chunks, axis=0) + dfull_ref[...] * sx
    y_ssd = _rms_norm(y * jax.nn.silu(z), snorm_ref[...]).astype(BF16)

    mix = _mm(y_lru, wout_l_ref[...]) + _mm(y_ssd, wout_s_ref[...])
    o_ref[0] = xb + _rms_norm(mix, post_w_ref[...])


def _ffn_kernel(x_ref, pre_w_ref, wg_ref, wu_ref, wd_ref, post_w_ref, o_ref):
    xb = x_ref[...]
    hb = _rms_norm(xb, pre_w_ref[...]).astype(BF16)
    g = _mm(hb, wg_ref[...])
    up = _mm(hb, wu_ref[...])
    act = (jax.nn.silu(g) * up).astype(BF16)
    f = _mm(act, wd_ref[...])
    o_ref[...] = xb + _rms_norm(f, post_w_ref[...])


def _const_spec(shape):
    zeros = (0,) * len(shape)
    return pl.BlockSpec(shape, lambda *_: zeros, pipeline_mode=pl.Buffered(1))


def _vmem_limit(resident_bytes, streamed_bytes, temp_bytes):
    need = resident_bytes + 2 * streamed_bytes + temp_bytes
    assert need <= V7X_VMEM_BYTES, need
    return need


def _nbytes(*arrays):
    return sum(math.prod(a.shape) * a.dtype.itemsize for a in arrays)


def _block_diag_gates(wa, wx):
    def bd(w):
        w4 = w.reshape(LRU_NGROUPS, LRU_GROUP, LRU_BLOCK_W, LRU_BLOCK_W)
        eye = jnp.eye(LRU_GROUP, dtype=w.dtype)
        full = jnp.einsum("gkij,kl->gkilj", w4, eye)
        return full.reshape(LRU_NGROUPS, V7X_MXU_DIM, V7X_MXU_DIM)
    return jnp.concatenate([bd(wa), bd(wx)], axis=-1)


def _mixer(x, consts, block_t):
    bsz, T, _ = x.shape
    assert T % block_t == 0 and block_t % SSD_CHUNK == 0
    x_spec = pl.BlockSpec((1, block_t, D_MODEL), lambda b, j: (b, j, 0))
    scratch = [
        pltpu.VMEM((block_t + V7X_SUBLANES, LRU_WIDTH), F32),
        pltpu.VMEM((block_t + V7X_SUBLANES, SSD_CONV_CH), F32),
        pltpu.VMEM((V7X_SUBLANES, LRU_WIDTH), F32),
        pltpu.VMEM((SSD_PAIRS, SSD_STATE, V7X_LANES), F32),
    ]
    block_bytes = block_t * D_MODEL * 4
    scratch_bytes = (2 * (block_t + V7X_SUBLANES) * (LRU_WIDTH + SSD_CONV_CH) * 4 // 2
                     + V7X_SUBLANES * LRU_WIDTH * 4 + SSD_PAIRS * SSD_STATE * V7X_LANES * 4)
    temp_bytes = 24 * block_bytes
    limit = _vmem_limit(_nbytes(*consts) + scratch_bytes, 2 * block_bytes, temp_bytes)
    return pl.pallas_call(
        _mixer_kernel,
        grid=(bsz, T // block_t),
        in_specs=[x_spec] + [_const_spec(c.shape) for c in consts],
        out_specs=x_spec,
        out_shape=jax.ShapeDtypeStruct(x.shape, x.dtype),
        scratch_shapes=scratch,
        compiler_params=pltpu.CompilerParams(
            dimension_semantics=("arbitrary", "arbitrary"), vmem_limit_bytes=limit),
        name="mixer",
    )(x, *consts)


def _ffn(x2d, consts, block_m):
    M, _ = x2d.shape
    assert M % block_m == 0
    x_spec = pl.BlockSpec((block_m, D_MODEL), lambda i: (i, 0))
    block_bytes = block_m * D_MODEL * 4
    temp_bytes = 4 * block_m * D_FF * 4 + 4 * block_bytes
    limit = _vmem_limit(_nbytes(*consts), 2 * block_bytes, temp_bytes)
    return pl.pallas_call(
        _ffn_kernel,
        grid=(M // block_m,),
        in_specs=[x_spec] + [_const_spec(c.shape) for c in consts],
        out_specs=x_spec,
        out_shape=jax.ShapeDtypeStruct(x2d.shape, x2d.dtype),
        compiler_params=pltpu.CompilerParams(
            dimension_semantics=("arbitrary",), vmem_limit_bytes=limit),
        name="ffn",
    )(x2d, *consts)


def kernel(x, pre_mix_norm, w_in, lru_conv_w, lru_conv_b, lru_wa, lru_ba, lru_wx, lru_bx, lru_lambda, lru_out_norm, ssd_conv_w, ssd_conv_b, ssd_dt_bias, ssd_a_log, ssd_d, ssd_out_norm, w_out, post_mix_norm, pre_ffn_norm, w_gate, w_up, w_down, post_ffn_norm):
    bsz, T, _ = x.shape
    depth = w_in.shape[0]
    row = lambda v: v.reshape(1, -1).astype(F32)
    pad_heads = lambda v: jnp.pad(v.reshape(1, -1).astype(F32), ((0, 0), (0, V7X_LANES - SSD_HEADS)))
    o1, o2, o3, o4 = LRU_WIDTH, 2 * LRU_WIDTH, 2 * LRU_WIDTH + SSD_INNER, 2 * LRU_WIDTH + SSD_INNER + SSD_CONV_CH
    for li in range(depth):
        w_in_b = w_in[li].astype(BF16)
        w_out_b = w_out[li].astype(BF16)
        mixer_consts = (
            row(pre_mix_norm[li]),
            w_in_b[:, :o1], w_in_b[:, o1:o2], w_in_b[:, o2:o3], w_in_b[:, o3:o4],
            jnp.pad(w_in_b[:, o4:], ((0, 0), (0, V7X_LANES - SSD_HEADS))),
            lru_conv_w[li].astype(F32), row(lru_conv_b[li]),
            _block_diag_gates(lru_wa[li], lru_wx[li]).astype(BF16),
            row(lru_ba[li]), row(lru_bx[li]), row(lru_lambda[li]), row(lru_out_norm[li]),
            ssd_conv_w[li].astype(F32), row(ssd_conv_b[li]),
            pad_heads(ssd_dt_bias[li]), pad_heads(ssd_a_log[li]),
            row(jnp.repeat(ssd_d[li], SSD_HEAD_DIM)), row(ssd_out_norm[li]),
            w_out_b[:LRU_WIDTH], w_out_b[LRU_WIDTH:], row(post_mix_norm[li]),
        )
        x = _mixer(x, mixer_consts, MIX_BLOCK_T)
        ffn_consts = (
            row(pre_ffn_norm[li]), w_gate[li].astype(BF16), w_up[li].astype(BF16),
            w_down[li].astype(BF16), row(post_ffn_norm[li]),
        )
        x = _ffn(x.reshape(bsz * T, D_MODEL), ffn_consts, FFN_BLOCK_M).reshape(bsz, T, D_MODEL)
    return x
```

```python
import functools
import math

import jax
import jax.numpy as jnp
from jax import lax
from jax.experimental import pallas as pl
from jax.experimental.pallas import tpu as pltpu

D_MODEL = 1024
LRU_WIDTH = D_MODEL
LRU_BLOCKS = 16
LRU_BLOCK_W = LRU_WIDTH // LRU_BLOCKS
LRU_C = 8.0
CONV_K = 4
SSD_HEAD_DIM = 64
SSD_INNER = D_MODEL
SSD_HEADS = SSD_INNER // SSD_HEAD_DIM
SSD_GROUPS = 2
SSD_STATE = 128
SSD_CHUNK = 128
SSD_BC = SSD_GROUPS * SSD_STATE
SSD_CONV_CH = SSD_INNER + 2 * SSD_BC
MIX_WIDTH = LRU_WIDTH + SSD_INNER
D_FF = -(-8 * D_MODEL // (3 * 256)) * 256
EPS = 1e-6

V7X_LANES = 128
V7X_SUBLANES = 8
V7X_MXU_DIM = 256
V7X_VMEM_BYTES = 64 * 1024 * 1024

LRU_GROUP = V7X_MXU_DIM // LRU_BLOCK_W
LRU_NGROUPS = LRU_BLOCKS // LRU_GROUP
HEADS_PER_TILE = V7X_LANES // SSD_HEAD_DIM
SSD_PAIRS = SSD_HEADS // HEADS_PER_TILE
PAIRS_PER_GROUP = SSD_PAIRS // SSD_GROUPS
FFN_SLABS = D_FF // V7X_MXU_DIM
FFN_DOWN_SPLITS = (0, 4, 8, FFN_SLABS)

BLOCK_T = 256

BF16 = jnp.bfloat16
F32 = jnp.float32


def _rms_norm(x, w):
    y = x * lax.rsqrt(jnp.mean(x * x, axis=-1, keepdims=True) + EPS)
    return y * w


def _softplus(x):
    return jnp.maximum(x, 0.0) + jnp.log1p(jnp.exp(-jnp.abs(x)))


def _mm(a, b):
    return jnp.dot(a, b, preferred_element_type=F32)


def _split3(x):
    hi = x.astype(BF16)
    r1 = x - hi.astype(F32)
    mid = r1.astype(BF16)
    lo = (r1 - mid.astype(F32)).astype(BF16)
    return hi, mid, lo


def _causal_conv(buf_ref, w_ref, b_ref, n):
    base = V7X_SUBLANES - (CONV_K - 1)
    y = b_ref[...] + buf_ref[pl.ds(V7X_SUBLANES, n), :] * w_ref[CONV_K - 1:CONV_K, :]
    for k in range(CONV_K - 1):
        y = y + buf_ref[pl.ds(base + k, n), :] * w_ref[k:k + 1, :]
    return y


def _lru_scan(a, u, h0):
    n, w = a.shape
    nt = n // V7X_SUBLANES
    a3 = a.reshape(nt, V7X_SUBLANES, w)
    u3 = u.reshape(nt, V7X_SUBLANES, w)
    row = lax.broadcasted_iota(jnp.int32, (1, V7X_SUBLANES, w), 1)
    d = 1
    while d < V7X_SUBLANES:
        m = row >= d
        a_sh = pltpu.roll(a3, d, axis=1)
        u_sh = pltpu.roll(u3, d, axis=1)
        u3 = u3 + a3 * jnp.where(m, u_sh, 0.0)
        a3 = a3 * jnp.where(m, a_sh, 1.0)
        d *= 2
    tiles = []
    h = h0
    for i in range(nt):
        ht = u3[i] + a3[i] * h
        tiles.append(ht)
        h = ht[V7X_SUBLANES - 1:V7X_SUBLANES, :]
    return jnp.concatenate(tiles, axis=0), h


def _layer_kernel(x_ref, pre_w_ref, w_lx_ref, w_lg_ref, w_z_ref, w_xbc_ref, w_dt_ref,
                  lconv_w_ref, lconv_b_ref, w_gates_ref, ba_ref, bx_ref, lam_ref, lnorm_ref,
                  sconv_w_ref, sconv_b_ref, dtb_ref, alog_ref, dfull_ref, snorm_ref,
                  wout_l_ref, wout_s_ref, post_w_ref,
                  fpre_w_ref, wg_ref, wu_ref, wd_ref, fpost_w_ref,
                  o_ref,
                  lxbuf, xbcbuf, hcarry, state, ybuf, xprev, *, n_tblocks):
    n = x_ref.shape[1]
    s = pl.program_id(0)
    j = lax.rem(s, n_tblocks)

    @pl.when(s == 0)
    def _():
        ybuf[...] = jnp.zeros_like(ybuf)
        xprev[...] = jnp.zeros_like(xprev)

    tail = pl.ds(n, V7X_SUBLANES)
    head = pl.ds(0, V7X_SUBLANES)

    @pl.when(j == 0)
    def _():
        lxbuf[head, :] = jnp.zeros((V7X_SUBLANES, LRU_WIDTH), F32)
        xbcbuf[head, :] = jnp.zeros((V7X_SUBLANES, SSD_CONV_CH), F32)
        hcarry[...] = jnp.zeros_like(hcarry)
        state[...] = jnp.zeros_like(state)

    @pl.when(j > 0)
    def _():
        lxbuf[head, :] = lxbuf[tail, :]
        xbcbuf[head, :] = xbcbuf[tail, :]

    acts = []
    downs = []

    xq = xprev[...]
    mix = _mm(ybuf[:, :LRU_WIDTH], wout_l_ref[...]) + _mm(ybuf[:, LRU_WIDTH:], wout_s_ref[...])
    xp = xq + _rms_norm(mix, post_w_ref[...])
    hp = _rms_norm(xp, fpre_w_ref[...]).astype(BF16)

    def ffn_slabs(lo_c, hi_c):
        for c in range(lo_c, hi_c):
            cols = slice(c * V7X_MXU_DIM, (c + 1) * V7X_MXU_DIM)
            g = _mm(hp, wg_ref[:, cols])
            up = _mm(hp, wu_ref[:, cols])
            acts.append((jax.nn.silu(g) * up).astype(BF16))
            if c + 1 in FFN_DOWN_SPLITS:
                lo_d = FFN_DOWN_SPLITS[FFN_DOWN_SPLITS.index(c + 1) - 1]
                rows = slice(lo_d * V7X_MXU_DIM, (c + 1) * V7X_MXU_DIM)
                downs.append(_mm(jnp.concatenate(acts[lo_d:c + 1], axis=1), wd_ref[rows, :]))

    xb = x_ref[0]
    hb = _rms_norm(xb, pre_w_ref[...]).astype(BF16)
    body = pl.ds(V7X_SUBLANES, n)
    lxbuf[body, :] = _mm(hb, w_lx_ref[...])
    lx = _causal_conv(lxbuf, lconv_w_ref, lconv_b_ref, n)
    lxb = lx.astype(BF16)
    ffn_slabs(0, 1)
    xbcbuf[body, :] = _mm(hb, w_xbc_ref[...])
    xbc = jax.nn.silu(_causal_conv(xbcbuf, sconv_w_ref, sconv_b_ref, n))
    gates = []
    for q in range(LRU_NGROUPS):
        lanes = slice(q * V7X_MXU_DIM, (q + 1) * V7X_MXU_DIM)
        gates.append(_mm(lxb[:, lanes], w_gates_ref[q]))
    ffn_slabs(1, 2)
    gate = _mm(hb, w_lg_ref[...])
    z = _mm(hb, w_z_ref[...])
    dt_raw = _mm(hb, w_dt_ref[...])

    ra = jnp.concatenate([g[:, :V7X_MXU_DIM] for g in gates], axis=1)
    rx = jnp.concatenate([g[:, V7X_MXU_DIM:] for g in gates], axis=1)
    r = jax.nn.sigmoid(ra + ba_ref[...])
    i_g = jax.nn.sigmoid(rx + bx_ref[...])
    log_a = (-LRU_C * r) * _softplus(-lam_ref[...])
    a = jnp.exp(log_a)
    one_minus_a2 = -jnp.tanh(log_a) * (a * a + 1.0)
    u = jnp.sqrt(one_minus_a2) * (i_g * lx)
    hseq, hlast = _lru_scan(a, u, hcarry[0:1, :])
    hcarry[...] = jnp.broadcast_to(hlast, hcarry.shape)
    y_lru = _rms_norm(hseq * jax.nn.gelu(gate), lnorm_ref[...]).astype(BF16)
    ffn_slabs(2, 4)

    sx = xbc[:, :SSD_INNER]
    sB = xbc[:, SSD_INNER:SSD_INNER + SSD_BC]
    sC = xbc[:, SSD_INNER + SSD_BC:]
    dt = _softplus(dt_raw + dtb_ref[...])
    head_lane = lax.broadcasted_iota(jnp.int32, (1, V7X_LANES), 1)
    a_neg = jnp.where(head_lane < SSD_HEADS, -jnp.exp(alog_ref[...]), 0.0)
    adt = dt * a_neg

    L = SSD_CHUNK
    ri = lax.broadcasted_iota(jnp.int32, (L, L), 0)
    ci = lax.broadcasted_iota(jnp.int32, (L, L), 1)
    tri = ri >= ci
    tri_b = tri.astype(BF16)
    lane = lax.broadcasted_iota(jnp.int32, (1, V7X_LANES), 1)
    lo_half = lane < SSD_HEAD_DIM

    n_chunks = n // L
    slots = [(c, p) for c in range(n_chunks) for p in range(1, SSD_PAIRS, 2)]
    fill = {}
    for k, c_ffn in enumerate(range(4, FFN_SLABS)):
        fill.setdefault(slots[k * len(slots) // (FFN_SLABS - 4)], []).append(c_ffn)
    y_chunks = []
    for c in range(n_chunks):
        rows = slice(c * L, (c + 1) * L)
        dt_c = dt[rows]
        hi, mid, lo = _split3(adt[rows])
        cs = _mm(tri_b, hi) + _mm(tri_b, mid) + _mm(tri_b, lo)
        cs_last = cs[L - 1:L, :]
        wcol = jnp.exp(cs_last - cs) * dt_c
        csT = cs.T
        dtT = dt_c.T
        wT = wcol.T
        sx_c = sx[rows]
        y_pairs = []
        for g in range(SSD_GROUPS):
            gl = slice(g * SSD_STATE, (g + 1) * SSD_STATE)
            B_g = sB[rows, gl]
            C_g = sC[rows, gl]
            BT_g = B_g.T
            scores = _mm(C_g.astype(BF16), BT_g.astype(BF16))
            for pp in range(PAIRS_PER_GROUP):
                p = g * PAIRS_PER_GROUP + pp
                sx_p = sx_c[:, p * V7X_LANES:(p + 1) * V7X_LANES]
                sx_lo = jnp.where(lo_half, sx_p, 0.0).astype(BF16)
                sx_hi = jnp.where(lo_half, 0.0, sx_p).astype(BF16)
                st = state[p]
                st_lo = jnp.where(lo_half, st, 0.0).astype(BF16)
                st_hi = jnp.where(lo_half, 0.0, st).astype(BF16)
                lhs_y, lhs_s, cdec = [], [], []
                for e in (HEADS_PER_TILE * p, HEADS_PER_TILE * p + 1):
                    colb = jnp.broadcast_to(cs[:, e:e + 1], (L, L))
                    lm = jnp.exp(colb - csT[e:e + 1, :])
                    m_e = jnp.where(tri, scores * lm, 0.0) * dtT[e:e + 1, :]
                    ecol = jnp.exp(colb)
                    lhs_y += [m_e.astype(BF16), (C_g * ecol).astype(BF16)]
                    lhs_s.append((BT_g * wT[e:e + 1, :]).astype(BF16))
                    cdec.append(ecol[L - 1:L, :])
                rhs_y = jnp.concatenate([sx_lo, st_lo, sx_hi, st_hi], axis=0)
                y_pairs.append(_mm(jnp.concatenate(lhs_y, axis=1), rhs_y))
                new = _mm(jnp.concatenate(lhs_s, axis=1), jnp.concatenate([sx_lo, sx_hi], axis=0))
                state[p] = st * jnp.where(lo_half, cdec[0], cdec[1]) + new
                for c_ffn in fill.get((c, p), ()):
                    ffn_slabs(c_ffn, c_ffn + 1)
        y_chunks.append(jnp.concatenate(y_pairs, axis=1))

    f = downs[0]
    for part in downs[1:]:
        f = f + part
    o_ref[0] = xp + _rms_norm(f, fpost_w_ref[...])

    y = jnp.concatenate(y_chunks, axis=0) + dfull_ref[...] * sx
    y_ssd = _rms_norm(y * jax.nn.silu(z), snorm_ref[...]).astype(BF16)
    ybuf[:, :LRU_WIDTH] = y_lru
    ybuf[:, LRU_WIDTH:] = y_ssd
    xprev[...] = xb


def _const_spec(shape):
    zeros = (0,) * len(shape)
    return pl.BlockSpec(shape, lambda *_: zeros, pipeline_mode=pl.Buffered(1))


def _nbytes(*arrays):
    return sum(math.prod(a.shape) * a.dtype.itemsize for a in arrays)


def _block_diag_gates(wa, wx):
    def bd(w):
        w4 = w.reshape(LRU_NGROUPS, LRU_GROUP, LRU_BLOCK_W, LRU_BLOCK_W)
        eye = jnp.eye(LRU_GROUP, dtype=w.dtype)
        full = jnp.einsum("gkij,kl->gkilj", w4, eye)
        return full.reshape(LRU_NGROUPS, V7X_MXU_DIM, V7X_MXU_DIM)
    return jnp.concatenate([bd(wa), bd(wx)], axis=-1)


def _layer(x, consts, block_t):
    bsz, T, _ = x.shape
    assert T % block_t == 0 and block_t % SSD_CHUNK == 0
    n_tblocks = T // block_t
    n_blocks = bsz * n_tblocks

    def in_map(s):
        q = jnp.minimum(s, n_blocks - 1)
        return (q // n_tblocks, q % n_tblocks, 0)

    def out_map(s):
        q = jnp.maximum(s - 1, 0)
        return (q // n_tblocks, q % n_tblocks, 0)

    scratch = [
        pltpu.VMEM((block_t + V7X_SUBLANES, LRU_WIDTH), F32),
        pltpu.VMEM((block_t + V7X_SUBLANES, SSD_CONV_CH), F32),
        pltpu.VMEM((V7X_SUBLANES, LRU_WIDTH), F32),
        pltpu.VMEM((SSD_PAIRS, SSD_STATE, V7X_LANES), F32),
        pltpu.VMEM((block_t, MIX_WIDTH), BF16),
        pltpu.VMEM((block_t, D_MODEL), F32),
    ]
    block_bytes = block_t * D_MODEL * 4
    scratch_bytes = ((block_t + V7X_SUBLANES) * (LRU_WIDTH + SSD_CONV_CH) * 4 + V7X_SUBLANES * LRU_WIDTH * 4
                     + SSD_PAIRS * SSD_STATE * V7X_LANES * 4 + block_t * MIX_WIDTH * 2 + block_bytes)
    temp_bytes = 14 * block_bytes
    vmem_limit = _nbytes(*consts) + scratch_bytes + 2 * 2 * block_bytes + temp_bytes
    assert vmem_limit <= V7X_VMEM_BYTES, vmem_limit
    return pl.pallas_call(
        functools.partial(_layer_kernel, n_tblocks=n_tblocks),
        grid=(n_blocks + 1,),
        in_specs=[pl.BlockSpec((1, block_t, D_MODEL), in_map)] + [_const_spec(c.shape) for c in consts],
        out_specs=pl.BlockSpec((1, block_t, D_MODEL), out_map),
        out_shape=jax.ShapeDtypeStruct(x.shape, x.dtype),
        scratch_shapes=scratch,
        compiler_params=pltpu.CompilerParams(
            dimension_semantics=("arbitrary",), vmem_limit_bytes=vmem_limit),
        name="layer",
    )(x, *consts)


def kernel(x, pre_mix_norm, w_in, lru_conv_w, lru_conv_b, lru_wa, lru_ba, lru_wx, lru_bx, lru_lambda, lru_out_norm, ssd_conv_w, ssd_conv_b, ssd_dt_bias, ssd_a_log, ssd_d, ssd_out_norm, w_out, post_mix_norm, pre_ffn_norm, w_gate, w_up, w_down, post_ffn_norm):
    depth = w_in.shape[0]
    row = lambda v: v.reshape(1, -1).astype(F32)
    pad_heads = lambda v: jnp.pad(v.reshape(1, -1).astype(F32), ((0, 0), (0, V7X_LANES - SSD_HEADS)))
    o1, o2, o3, o4 = LRU_WIDTH, 2 * LRU_WIDTH, 2 * LRU_WIDTH + SSD_INNER, 2 * LRU_WIDTH + SSD_INNER + SSD_CONV_CH
    for li in range(depth):
        w_in_b = w_in[li].astype(BF16)
        w_out_b = w_out[li].astype(BF16)
        consts = (
            row(pre_mix_norm[li]),
            w_in_b[:, :o1], w_in_b[:, o1:o2], w_in_b[:, o2:o3], w_in_b[:, o3:o4],
            jnp.pad(w_in_b[:, o4:], ((0, 0), (0, V7X_LANES - SSD_HEADS))),
            lru_conv_w[li].astype(F32), row(lru_conv_b[li]),
            _block_diag_gates(lru_wa[li], lru_wx[li]).astype(BF16),
            row(lru_ba[li]), row(lru_bx[li]), row(lru_lambda[li]), row(lru_out_norm[li]),
            ssd_conv_w[li].astype(F32), row(ssd_conv_b[li]),
            pad_heads(ssd_dt_bias[li]), pad_heads(ssd_a_log[li]),
            row(jnp.repeat(ssd_d[li], SSD_HEAD_DIM)), row(ssd_out_norm[li]),
            w_out_b[:LRU_WIDTH], w_out_b[LRU_WIDTH:], row(post_mix_norm[li]),
            row(pre_ffn_norm[li]), w_gate[li].astype(BF16), w_up[li].astype(BF16),
            w_down[li].astype(BF16), row(post_ffn_norm[li]),
        )
        x = _layer(x, consts, BLOCK_T)
    return x
```
